```python
import jax, jax.numpy as jnp
from jax import lax
import numpy as np

D_MODEL = 2048
BATCH = 8
SEQ = 2048
DEPTH = 1

N_META = 16
BLOCK_Q = 128
CONV_CH = 2048
CONV_WIDTH = 31
N_HEADS = 16
Q_LORA = 512
KV_LORA = 512
QK_NOPE = 128
QK_ROPE = 64
V_HEAD = 128
ROPE_THETA = 10000.0
D_FF = -(-8 * D_MODEL // (3 * 256)) * 256
EPS = 1e-6
SPLITS = (2 * CONV_CH, Q_LORA, KV_LORA, QK_ROPE, 2 * D_MODEL)
IN_COLS = sum(SPLITS)
SPLIT_IDX = tuple(int(v) for v in np.cumsum(SPLITS)[:-1])

kernel_name = "hybrid_conformer_conv_mla_gated_block"


def rmsnorm(x, g):
    xf = x.astype(jnp.float32)
    y = xf * lax.rsqrt(jnp.mean(xf * xf, axis=-1, keepdims=True) + EPS)
    return y.astype(x.dtype) * g


def layernorm(x, g, b):
    xf = x.astype(jnp.float32)
    mu = jnp.mean(xf, axis=-1, keepdims=True)
    var = jnp.mean(jnp.square(xf - mu), axis=-1, keepdims=True)
    return ((xf - mu) * lax.rsqrt(var + EPS)).astype(x.dtype) * g + b


def rope(x, cos, sin):
    x1, x2 = jnp.split(x, 2, axis=-1)
    return jnp.concatenate([x1 * cos - x2 * sin, x1 * sin + x2 * cos], axis=-1)


def causal_depthwise_conv(x, w):
    return lax.conv_general_dilated(
        x, w[:, None, :], window_strides=(1,), padding=[(CONV_WIDTH - 1, 0)],
        dimension_numbers=("NWC", "WIO", "NWC"), feature_group_count=x.shape[-1])


def mla_attention(q_nope, q_rope, k_nope, k_rope, v):
    b, lp = q_nope.shape[0], q_nope.shape[1]
    n_blocks = lp // BLOCK_Q
    scale = (QK_NOPE + QK_ROPE) ** -0.5
    kpos = jnp.arange(lp)

    def one_block(i):
        start = i * BLOCK_Q
        qn = lax.dynamic_slice_in_dim(q_nope, start, BLOCK_Q, axis=1)
        qr = lax.dynamic_slice_in_dim(q_rope, start, BLOCK_Q, axis=1)
        s = (jnp.einsum("bqhd,bkhd->bhqk", qn, k_nope)
             + jnp.einsum("bqhr,bkr->bhqk", qr, k_rope)).astype(jnp.float32) * scale
        qpos = start + jnp.arange(BLOCK_Q)
        s = jnp.where(kpos[None, :] <= qpos[:, None], s, -jnp.inf)
        p = jax.nn.softmax(s, axis=-1).astype(v.dtype)
        return jnp.einsum("bhqk,bkhd->bqhd", p, v)

    o = lax.map(one_block, jnp.arange(n_blocks))
    return jnp.moveaxis(o, 0, 1).reshape(b, lp, N_HEADS * V_HEAD)


def setup_inputs(seed: int = 0) -> dict:
    key = jax.random.key(seed)
    ks = iter(jax.random.split(key, 32))
    f32 = jnp.float32

    def w(shape, fan_in):
        return jax.random.normal(next(ks), shape, f32) * (fan_in ** -0.5)

    def gain(shape):
        return 1.0 + 0.02 * jax.random.normal(next(ks), shape, f32)

    def bias(shape):
        return 0.02 * jax.random.normal(next(ks), shape, f32)

    Ld = DEPTH
    return {
        "x": jax.random.normal(next(ks), (BATCH, SEQ, D_MODEL), f32),
        "meta_tokens": jax.random.normal(next(ks), (N_META, D_MODEL), f32),
        "g_mix": gain((Ld, D_MODEL)),
        "w_in": w((Ld, D_MODEL, IN_COLS), D_MODEL),
        "b_glu": bias((Ld, 2 * CONV_CH)),
        "b_gate": bias((Ld, 2 * D_MODEL)),
        "w_dw": w((Ld, CONV_WIDTH, CONV_CH), CONV_WIDTH),
        "b_dw": bias((Ld, CONV_CH)),
        "g_conv_ln": gain((Ld, CONV_CH)),
        "b_conv_ln": bias((Ld, CONV_CH)),
        "w_conv_out": w((Ld, CONV_CH, D_MODEL), CONV_CH),
        "b_conv_out": bias((Ld, D_MODEL)),
        "g_q_lora": gain((Ld, Q_LORA)),
        "w_uq": w((Ld, Q_LORA, N_HEADS * (QK_NOPE + QK_ROPE)), Q_LORA),
        "g_kv_lora": gain((Ld, KV_LORA)),
        "w_uk": w((Ld, KV_LORA, N_HEADS * QK_NOPE), KV_LORA),
        "w_uv": w((Ld, KV_LORA, N_HEADS * V_HEAD), KV_LORA),
        "w_attn_out": w((Ld, N_HEADS * V_HEAD, D_MODEL), N_HEADS * V_HEAD),
        "w_out": w((Ld, D_MODEL, D_MODEL), D_MODEL),
        "g_ffn": gain((Ld, D_MODEL)),
        "w_ffn_gate": w((Ld, D_MODEL, D_FF), D_MODEL),
        "w_ffn_up": w((Ld, D_MODEL, D_FF), D_MODEL),
        "w_ffn_down": w((Ld, D_FF, D_MODEL), D_FF),
        "g_final": gain((D_MODEL,)),
    }


def reference(x, meta_tokens, g_mix, w_in, b_glu, b_gate, w_dw, b_dw, g_conv_ln, b_conv_ln,
              w_conv_out, b_conv_out, g_q_lora, w_uq, g_kv_lora, w_uk, w_uv, w_attn_out,
              w_out, g_ffn, w_ffn_gate, w_ffn_up, w_ffn_down, g_final):
    b, seq, d = x.shape
    length = N_META + seq
    lp = -(-length // BLOCK_Q) * BLOCK_Q
    meta = jnp.broadcast_to(meta_tokens[None].astype(x.dtype), (b, N_META, d))
    h = jnp.concatenate([meta, x], axis=1)
    h = jnp.pad(h, ((0, 0), (0, lp - length), (0, 0)))

    pos = jnp.arange(lp, dtype=jnp.float32)
    inv_freq = ROPE_THETA ** (-jnp.arange(0, QK_ROPE, 2, dtype=jnp.float32) / QK_ROPE)
    ang = pos[:, None] * inv_freq[None, :]
    cos, sin = jnp.cos(ang).astype(h.dtype), jnp.sin(ang).astype(h.dtype)

    for l in range(DEPTH):
        u = rmsnorm(h, g_mix[l])
        z = u @ w_in[l]
        z_glu, z_cq, z_ckv, z_kr, z_gate = jnp.split(z, SPLIT_IDX, axis=-1)

        za, zb = jnp.split(z_glu + b_glu[l], 2, axis=-1)
        c = za * jax.nn.sigmoid(zb)
        c = causal_depthwise_conv(c, w_dw[l]) + b_dw[l]
        c = jax.nn.silu(layernorm(c, g_conv_ln[l], b_conv_ln[l]))
        y_conv = c @ w_conv_out[l] + b_conv_out[l]

        cq = rmsnorm(z_cq, g_q_lora[l])
        q = (cq @ w_uq[l]).reshape(b, lp, N_HEADS, QK_NOPE + QK_ROPE)
        q_nope = q[..., :QK_NOPE]
        q_rope = rope(q[..., QK_NOPE:], cos[:, None, :], sin[:, None, :])
        ckv = rmsnorm(z_ckv, g_kv_lora[l])
        k_nope = (ckv @ w_uk[l]).reshape(b, lp, N_HEADS, QK_NOPE)
        v = (ckv @ w_uv[l]).reshape(b, lp, N_HEADS, V_HEAD)
        k_rope = rope(z_kr, cos, sin)
        y_attn = mla_attention(q_nope, q_rope, k_nope, k_rope, v) @ w_attn_out[l]

        g_c, g_a = jnp.split(jax.nn.sigmoid(z_gate + b_gate[l]), 2, axis=-1)
        h = h + (g_c * y_conv + g_a * y_attn) @ w_out[l]

        hn = rmsnorm(h, g_ffn[l])
        h = h + (jax.nn.silu(hn @ w_ffn_gate[l]) * (hn @ w_ffn_up[l])) @ w_ffn_down[l]

    h = rmsnorm(h, g_final)
    return h[:, N_META:N_META + seq]
```

```python
import functools
import math

import jax
import jax.numpy as jnp
from jax import lax
from jax.experimental import pallas as pl
from jax.experimental.pallas import tpu as pltpu

D_MODEL = 2048
N_META = 16
CONV_WIDTH = 31
N_HEADS = 16
Q_LORA = 512
KV_LORA = 512
QK_NOPE = 128
QK_ROPE = 64
V_HEAD = 128
ROPE_THETA = 10000.0
EPS = 1e-6

HEAD_W = 2 * QK_NOPE
ROPE_HALF = QK_ROPE // 2
HALO = 32
META_ROWS = 128
NEG_BIG = -1e30

F32 = jnp.float32
BF16 = jnp.bfloat16

VMEM_LIMIT = 56 * 1024 * 1024


def _cparams(n_axes):
    return pltpu.CompilerParams(dimension_semantics=("arbitrary",) * n_axes,
                                vmem_limit_bytes=VMEM_LIMIT)


def _dot(a, b):
    return jnp.dot(a, b, preferred_element_type=F32)


def _sigmoid(x):
    return 1.0 / (1.0 + jnp.exp(-x))


def _rms_rows(x, g):
    ms = jnp.mean(x * x, axis=-1, keepdims=True)
    return (x * lax.rsqrt(ms + EPS)) * g


def _glu_kernel(x_ref, g_ref, wa_ref, wb_ref, ba_ref, bb_ref, u_ref, c_ref, u_scr, *, tm, rchunk):
    j = pl.program_id(1)

    @pl.when(j == 0)
    def _():
        def body(r, _):
            r0 = pl.multiple_of(r * rchunk, rchunk)
            ub = _rms_rows(x_ref[pl.ds(r0, rchunk), :], g_ref[...]).astype(BF16)
            u_scr[pl.ds(r0, rchunk), :] = ub
            u_ref[pl.ds(r0, rchunk), :] = ub
            return 0
        lax.fori_loop(0, tm // rchunk, body, 0)

    u = u_scr[...]
    za = _dot(u, wa_ref[...]) + ba_ref[...]
    zb = _dot(u, wb_ref[...]) + bb_ref[...]
    c_ref[...] = za * _sigmoid(zb)


def _glu_call(xr, g, wa, wb, ba, bb, *, tm, tn):
    m = xr.shape[0]
    d = D_MODEL
    grid = (m // tm, d // tn)
    return pl.pallas_call(
        functools.partial(_glu_kernel, tm=tm, rchunk=min(tm, 32)),
        grid=grid,
        in_specs=[
            pl.BlockSpec((tm, d), lambda i, j: (i, 0)),
            pl.BlockSpec((1, d), lambda i, j: (0, 0)),
            pl.BlockSpec((d, tn), lambda i, j: (0, j)),
            pl.BlockSpec((d, tn), lambda i, j: (0, j)),
            pl.BlockSpec((1, tn), lambda i, j: (0, j)),
            pl.BlockSpec((1, tn), lambda i, j: (0, j)),
        ],
        out_specs=[
            pl.BlockSpec((tm, d), lambda i, j: (i, 0)),
            pl.BlockSpec((tm, tn), lambda i, j: (i, j)),
        ],
        out_shape=[
            jax.ShapeDtypeStruct((m, d), BF16),
            jax.ShapeDtypeStruct((m, d), F32),
        ],
        scratch_shapes=[pltpu.VMEM((tm, d), BF16)],
        compiler_params=_cparams(2),
        name="glu_in_proj",
    )(xr, g, wa, wb, ba, bb)


def _gate_kernel(u_ref, w_ref, b_ref, o_ref):
    o_ref[...] = _sigmoid(_dot(u_ref[...], w_ref[...]) + b_ref[...]).astype(o_ref.dtype)


def _gate_call(u, w, b, *, tm, tn):
    m, d = u.shape
    n = w.shape[1]
    return pl.pallas_call(
        _gate_kernel,
        grid=(m // tm, n // tn),
        in_specs=[
            pl.BlockSpec((tm, d), lambda i, j: (i, 0)),
            pl.BlockSpec((d, tn), lambda i, j: (0, j)),
            pl.BlockSpec((1, tn), lambda i, j: (0, j)),
        ],
        out_specs=pl.BlockSpec((tm, tn), lambda i, j: (i, j)),
        out_shape=jax.ShapeDtypeStruct((m, n), BF16),
        compiler_params=_cparams(2),
        name="gate_proj",
    )(u, w, b)


def _rope128(v, cos_t, sin_t):
    return v * cos_t + pltpu.roll(v, 2 * ROPE_HALF, 1) * sin_t


def _latent_kernel(u_ref, w_ref, gq_ref, gkv_ref, cos_ref, sin_ref, lat_ref, kr_ref):
    z = _dot(u_ref[...], w_ref[...])
    lat_ref[:, 0:Q_LORA] = _rms_rows(z[:, 0:Q_LORA], gq_ref[...]).astype(BF16)
    lat_ref[:, Q_LORA:Q_LORA + KV_LORA] = _rms_rows(
        z[:, Q_LORA:Q_LORA + KV_LORA], gkv_ref[...]).astype(BF16)
    kr = z[:, Q_LORA + KV_LORA:Q_LORA + KV_LORA + 128]
    kr_ref[...] = _rope128(kr, cos_ref[...], sin_ref[...]).astype(BF16)


def _latent_call(u, w, gq, gkv, cos_t, sin_t, *, tm):
    m, d = u.shape
    n = w.shape[1]
    n_tab = cos_t.shape[0] // tm
    return pl.pallas_call(
        _latent_kernel,
        grid=(m // tm,),
        in_specs=[
            pl.BlockSpec((tm, d), lambda i: (i, 0)),
            pl.BlockSpec((d, n), lambda i: (0, 0)),
            pl.BlockSpec((1, Q_LORA), lambda i: (0, 0)),
            pl.BlockSpec((1, KV_LORA), lambda i: (0, 0)),
            pl.BlockSpec((tm, 128), lambda i: (i % n_tab, 0)),
            pl.BlockSpec((tm, 128), lambda i: (i % n_tab, 0)),
        ],
        out_specs=[
            pl.BlockSpec((tm, Q_LORA + KV_LORA), lambda i: (i, 0)),
            pl.BlockSpec((tm, 128), lambda i: (i, 0)),
        ],
        out_shape=[
            jax.ShapeDtypeStruct((m, Q_LORA + KV_LORA), BF16),
            jax.ShapeDtypeStruct((m, 128), BF16),
        ],
        compiler_params=_cparams(1),
        name="latent_proj",
    )(u, w, gq, gkv, cos_t, sin_t)


Q_COLS = N_HEADS * HEAD_W
KV_COLS = N_HEADS * QK_NOPE
QKV_TN = 1024
Q_STEPS = Q_COLS // QKV_TN


def _qkv_kernel(lat_ref, w_ref, cos_ref, sin_ref, o_ref):
    j = pl.program_id(1)

    @pl.when(j < Q_STEPS)
    def _():
        z = _dot(lat_ref[:, 0:Q_LORA], w_ref[...])
        cos_t = cos_ref[...]
        sin_t = sin_ref[...]
        for hh in range(QKV_TN // HEAD_W):
            c0 = hh * HEAD_W
            o_ref[:, c0:c0 + QK_NOPE] = z[:, c0:c0 + QK_NOPE].astype(BF16)
            o_ref[:, c0 + QK_NOPE:c0 + HEAD_W] = _rope128(
                z[:, c0 + QK_NOPE:c0 + HEAD_W], cos_t, sin_t).astype(BF16)

    @pl.when(j >= Q_STEPS)
    def _():
        o_ref[...] = _dot(lat_ref[:, Q_LORA:Q_LORA + KV_LORA], w_ref[...]).astype(BF16)


def _qkv_call(lat, w, cos_t, sin_t, *, tm):
    m = lat.shape[0]
    n = w.shape[1]
    n_tab = cos_t.shape[0] // tm
    return pl.pallas_call(
        _qkv_kernel,
        grid=(m // tm, n // QKV_TN),
        in_specs=[
            pl.BlockSpec((tm, Q_LORA + KV_LORA), lambda i, j: (i, 0)),
            pl.BlockSpec((Q_LORA, QKV_TN), lambda i, j: (0, j)),
            pl.BlockSpec((tm, 128), lambda i, j: (i % n_tab, 0)),
            pl.BlockSpec((tm, 128), lambda i, j: (i % n_tab, 0)),
        ],
        out_specs=pl.BlockSpec((tm, QKV_TN), lambda i, j: (i, j)),
        out_shape=jax.ShapeDtypeStruct((m, n), BF16),
        compiler_params=_cparams(2),
        name="qkv_up_proj",
    )(lat, w, cos_t, sin_t)


CONV_R = 64
LN_R = 32
CONV_TN = 512


def _conv_kernel(cur_ref, prev_ref, halo0_ref, w8_ref, bdw_ref, gln_ref, bln_ref, wco_ref, bco_ref,
                 gc_ref, o_ref, xw_scr, conv_scr, c2_scr, *, tseq):
    i = pl.program_id(1)

    @pl.when(i == 0)
    def _():
        xw_scr[0:HALO, :] = halo0_ref[...]

    @pl.when(i > 0)
    def _():
        xw_scr[0:HALO, :] = prev_ref[...]

    xw_scr[HALO:HALO + tseq, :] = cur_ref[...]

    off = HALO - (CONV_WIDTH - 1)

    def col_body(cc, _):
        c0 = pl.multiple_of(cc * 128, 128)

        def row_body(rb, _):
            r0 = pl.multiple_of(rb * CONV_R, CONV_R)
            xv = xw_scr[pl.ds(r0, CONV_R + HALO), pl.ds(c0, 128)]
            acc = jnp.zeros((CONV_R // 8, 8, 128), F32)
            for jj in range(CONV_WIDTH):
                xs = xv[off + jj:off + jj + CONV_R]
                acc = acc + xs.reshape(CONV_R // 8, 8, 128) * w8_ref[jj, :, pl.ds(c0, 128)][None]
            conv_scr[pl.ds(r0, CONV_R), pl.ds(c0, 128)] = acc.reshape(CONV_R, 128)
            return 0

        lax.fori_loop(0, tseq // CONV_R, row_body, 0)
        return 0

    lax.fori_loop(0, D_MODEL // 128, col_body, 0)

    def ln_body(r, _):
        r0 = pl.multiple_of(r * LN_R, LN_R)
        c = conv_scr[pl.ds(r0, LN_R), :] + bdw_ref[...]
        mu = jnp.mean(c, axis=-1, keepdims=True)
        cc = c - mu
        var = jnp.mean(cc * cc, axis=-1, keepdims=True)
        y = cc * lax.rsqrt(var + EPS) * gln_ref[...] + bln_ref[...]
        c2_scr[pl.ds(r0, LN_R), :] = (y * _sigmoid(y)).astype(BF16)
        return 0

    lax.fori_loop(0, tseq // LN_R, ln_body, 0)

    c2 = c2_scr[...]
    for n0 in range(0, D_MODEL, CONV_TN):
        y = _dot(c2, wco_ref[:, n0:n0 + CONV_TN]) + bco_ref[:, n0:n0 + CONV_TN]
        o_ref[:, n0:n0 + CONV_TN] = gc_ref[:, n0:n0 + CONV_TN].astype(F32) * y


def _conv_call(c_pre, halo0, w8, bdw, gln, bln, wco, bco, gates, *, batch, seq, tseq):
    d = D_MODEL
    nt = seq // tseq
    per_halo = tseq // HALO
    return pl.pallas_call(
        functools.partial(_conv_kernel, tseq=tseq),
        grid=(batch, nt),
        in_specs=[
            pl.BlockSpec((tseq, d), lambda b, i: (b * nt + i, 0)),
            pl.BlockSpec((HALO, d), lambda b, i: (jnp.maximum((b * nt + i) * per_halo - 1, 0), 0)),
            pl.BlockSpec((HALO, d), lambda b, i: (0, 0)),
            pl.BlockSpec((CONV_WIDTH, 8, d), lambda b, i: (0, 0, 0)),
            pl.BlockSpec((1, d), lambda b, i: (0, 0)),
            pl.BlockSpec((1, d), lambda b, i: (0, 0)),
            pl.BlockSpec((1, d), lambda b, i: (0, 0)),
            pl.BlockSpec((d, d), lambda b, i: (0, 0)),
            pl.BlockSpec((1, d), lambda b, i: (0, 0)),
            pl.BlockSpec((tseq, d), lambda b, i: (b * nt + i, 0)),
        ],
        out_specs=pl.BlockSpec((tseq, d), lambda b, i: (b * nt + i, 0)),
        out_shape=jax.ShapeDtypeStruct((batch * seq, d), F32),
        scratch_shapes=[
            pltpu.VMEM((tseq + HALO, d), F32),
            pltpu.VMEM((tseq, d), F32),
            pltpu.VMEM((tseq, d), BF16),
        ],
        compiler_params=_cparams(2),
        name="conv_module",
    )(c_pre, c_pre, halo0, w8, bdw, gln, bln, wco, bco, gates)


def _attn_kernel(q_ref, kn_ref, kr_ref, v_ref, knm_ref, krm_ref, vm_ref, o_ref,
                 m_scr, l_scr, acc_scr, *, tq, tk):
    i = pl.program_id(2)
    q = q_ref[...]
    nt_dims = (((1,), (1,)), ((), ()))

    km = jnp.concatenate([knm_ref[...], krm_ref[...]], axis=1)
    s = lax.dot_general(q, km, nt_dims, preferred_element_type=F32)
    col = lax.broadcasted_iota(jnp.int32, s.shape, 1)
    s = jnp.where(col < N_META, s, NEG_BIG)
    m0 = jnp.max(s, axis=1, keepdims=True)
    p = jnp.exp2(s - m0)
    m_scr[...] = m0
    l_scr[...] = jnp.sum(p, axis=1, keepdims=True)
    acc_scr[...] = _dot(p.astype(BF16), vm_ref[...])

    def step(t, masked):
        k0 = pl.multiple_of(t * tk, tk)
        k = jnp.concatenate([kn_ref[pl.ds(k0, tk), :], kr_ref[pl.ds(k0, tk), :]], axis=1)
        s = lax.dot_general(q, k, nt_dims, preferred_element_type=F32)
        if masked:
            qpos = i * tq + lax.broadcasted_iota(jnp.int32, s.shape, 0)
            kpos = k0 + lax.broadcasted_iota(jnp.int32, s.shape, 1)
            s = jnp.where(kpos <= qpos, s, NEG_BIG)
        m_prev = m_scr[...]
        m_new = jnp.maximum(m_prev, jnp.max(s, axis=1, keepdims=True))
        alpha = jnp.exp2(m_prev - m_new)
        p = jnp.exp2(s - m_new)
        l_scr[...] = alpha * l_scr[...] + jnp.sum(p, axis=1, keepdims=True)
        acc_scr[...] = alpha * acc_scr[...] + _dot(p.astype(BF16), v_ref[pl.ds(k0, tk), :])
        m_scr[...] = m_new

    per = tq // tk

    def full_body(t, _):
        step(t, False)
        return 0

    lax.fori_loop(0, i * per, full_body, 0)
    for dd in range(per):
        step(i * per + dd, True)

    o_ref[...] = (acc_scr[...] / l_scr[...]).astype(o_ref.dtype)


def _attn_call(qkv, kr, qkv_m, kr_m, *, batch, seq, tq, tk):
    nq = seq // tq
    kn_blk0 = Q_COLS // QK_NOPE
    v_blk0 = (Q_COLS + KV_COLS) // V_HEAD
    return pl.pallas_call(
        functools.partial(_attn_kernel, tq=tq, tk=tk),
        grid=(batch, N_HEADS, nq),
        in_specs=[
            pl.BlockSpec((tq, HEAD_W), lambda b, h, i: (b * nq + i, h)),
            pl.BlockSpec((seq, QK_NOPE), lambda b, h, i: (b, kn_blk0 + h)),
            pl.BlockSpec((seq, 128), lambda b, h, i: (b, 0)),
            pl.BlockSpec((seq, V_HEAD), lambda b, h, i: (b, v_blk0 + h)),
            pl.BlockSpec((META_ROWS, QK_NOPE), lambda b, h, i: (0, kn_blk0 + h)),
            pl.BlockSpec((META_ROWS, 128), lambda b, h, i: (0, 0)),
            pl.BlockSpec((META_ROWS, V_HEAD), lambda b, h, i: (0, v_blk0 + h)),
        ],
        out_specs=pl.BlockSpec((tq, V_HEAD), lambda b, h, i: (b * nq + i, h)),
        out_shape=jax.ShapeDtypeStruct((batch * seq, N_HEADS * V_HEAD), BF16),
        scratch_shapes=[
            pltpu.VMEM((tq, 1), F32),
            pltpu.VMEM((tq, 1), F32),
            pltpu.VMEM((tq, V_HEAD), F32),
        ],
        compiler_params=_cparams(3),
        name="mla_flash_attention",
    )(qkv, qkv, kr, qkv, qkv_m, kr_m, qkv_m)


def _mix_kernel(o_ref, w_ref, yc_ref, ga_ref, m_ref):
    ya = _dot(o_ref[...], w_ref[...])
    m_ref[...] = (yc_ref[...] + ga_ref[...].astype(F32) * ya).astype(m_ref.dtype)


def _mix_call(o, w, yc, gates, *, tm, tn):
    m, d = o.shape
    n = w.shape[1]
    ga_blk0 = D_MODEL // tn
    return pl.pallas_call(
        _mix_kernel,
        grid=(m // tm, n // tn),
        in_specs=[
            pl.BlockSpec((tm, d), lambda i, j: (i, 0)),
            pl.BlockSpec((d, tn), lambda i, j: (0, j)),
            pl.BlockSpec((tm, tn), lambda i, j: (i, j)),
            pl.BlockSpec((tm, tn), lambda i, j: (i, ga_blk0 + j)),
        ],
        out_specs=pl.BlockSpec((tm, tn), lambda i, j: (i, j)),
        out_shape=jax.ShapeDtypeStruct((m, n), BF16),
        compiler_params=_cparams(2),
        name="attn_out_mix",
    )(o, w, yc, gates)


def _resid_kernel(m_ref, w_ref, x_ref, h_ref):
    h_ref[...] = x_ref[...] + _dot(m_ref[...], w_ref[...])


def _resid_call(mixed, w, xr, *, tm, tn):
    m, d = mixed.shape
    n = w.shape[1]
    return pl.pallas_call(
        _resid_kernel,
        grid=(m // tm, n // tn),
        in_specs=[
            pl.BlockSpec((tm, d), lambda i, j: (i, 0)),
            pl.BlockSpec((d, tn), lambda i, j: (0, j)),
            pl.BlockSpec((tm, tn), lambda i, j: (i, j)),
        ],
        out_specs=pl.BlockSpec((tm, tn), lambda i, j: (i, j)),
        out_shape=jax.ShapeDtypeStruct((m, n), F32),
        compiler_params=_cparams(2),
        name="mixer_out_resid",
    )(mixed, w, xr)


def _ffn_kernel(h_ref, gf_ref, wg_ref, wu_ref, wd_ref, gfin_ref, o_ref, hn_scr, acc_scr, *, tm, rchunk):
    f = pl.program_id(1)
    nf = pl.num_programs(1)

    @pl.when(f == 0)
    def _():
        def body(r, _):
            r0 = pl.multiple_of(r * rchunk, rchunk)
            hn_scr[pl.ds(r0, rchunk), :] = _rms_rows(h_ref[pl.ds(r0, rchunk), :], gf_ref[...]).astype(BF16)
            return 0
        lax.fori_loop(0, tm // rchunk, body, 0)

    hn = hn_scr[...]
    g = _dot(hn, wg_ref[...])
    u = _dot(hn, wu_ref[...])
    a = (g * _sigmoid(g) * u).astype(BF16)
    part = _dot(a, wd_ref[...])

    @pl.when(f == 0)
    def _():
        acc_scr[...] = part

    @pl.when(f > 0)
    def _():
        acc_scr[...] += part

    @pl.when(f == nf - 1)
    def _():
        def body(r, _):
            r0 = pl.multiple_of(r * rchunk, rchunk)
            h2 = h_ref[pl.ds(r0, rchunk), :] + acc_scr[pl.ds(r0, rchunk), :]
            o_ref[pl.ds(r0, rchunk), :] = _rms_rows(h2, gfin_ref[...])
            return 0
        lax.fori_loop(0, tm // rchunk, body, 0)


def _ffn_call(h1, gf, wg, wu, wd, gfin, *, tm, tf):
    m, d = h1.shape
    dff = wg.shape[1]
    return pl.pallas_call(
        functools.partial(_ffn_kernel, tm=tm, rchunk=32),
        grid=(m // tm, dff // tf),
        in_specs=[
            pl.BlockSpec((tm, d), lambda i, f: (i, 0)),
            pl.BlockSpec((1, d), lambda i, f: (0, 0)),
            pl.BlockSpec((d, tf), lambda i, f: (0, f)),
            pl.BlockSpec((d, tf), lambda i, f: (0, f)),
            pl.BlockSpec((tf, d), lambda i, f: (f, 0)),
            pl.BlockSpec((1, d), lambda i, f: (0, 0)),
        ],
        out_specs=pl.BlockSpec((tm, d), lambda i, f: (i, 0)),
        out_shape=jax.ShapeDtypeStruct((m, d), F32),
        scratch_shapes=[
            pltpu.VMEM((tm, d), BF16),
            pltpu.VMEM((tm, d), F32),
        ],
        compiler_params=_cparams(2),
        name="swiglu_ffn",
    )(h1, gf, wg, wu, wd, gfin)


def _rope_tables(n_pos):
    pos = jnp.arange(n_pos, dtype=F32)
    inv_freq = ROPE_THETA ** (-jnp.arange(0, QK_ROPE, 2, dtype=F32) / QK_ROPE)
    ang = pos[:, None] * inv_freq[None, :]
    cos, sin = jnp.cos(ang), jnp.sin(ang)
    zero = jnp.zeros_like(cos)
    cos_t = jnp.concatenate([cos, cos, zero, zero], axis=1)
    sin_t = jnp.concatenate([-sin, sin, zero, zero], axis=1)
    return cos_t, sin_t


def kernel(x, meta_tokens, g_mix, w_in, b_glu, b_gate, w_dw, b_dw, g_conv_ln, b_conv_ln, w_conv_out, b_conv_out, g_q_lora, w_uq, g_kv_lora, w_uk, w_uv, w_attn_out, w_out, g_ffn, w_ffn_gate, w_ffn_up, w_ffn_down, g_final):
    batch, seq, d = x.shape
    assert d == D_MODEL and w_in.shape[0] == 1
    n_rows = batch * seq
    xr = x.reshape(n_rows, d)
    meta_pad = jnp.zeros((META_ROWS, d), F32).at[:N_META].set(meta_tokens.astype(F32))

    c_glu = 2 * d
    w0 = w_in[0]
    wa = w0[:, 0:d].astype(BF16)
    wb = w0[:, d:c_glu].astype(BF16)
    o_cq = c_glu
    o_ckv = o_cq + Q_LORA
    o_kr = o_ckv + KV_LORA
    o_gate = o_kr + QK_ROPE
    wkr1 = w0[:, o_kr:o_kr + ROPE_HALF]
    wkr2 = w0[:, o_kr + ROPE_HALF:o_gate]
    w_lat = jnp.concatenate([w0[:, o_cq:o_kr], wkr1, wkr2, wkr2, wkr1], axis=1).astype(BF16)
    w_gate = w0[:, o_gate:o_gate + 2 * d].astype(BF16)
    ba = b_glu[:, 0:d]
    bb = b_glu[:, d:c_glu]

    qk_scale = (QK_NOPE + QK_ROPE) ** -0.5 * math.log2(math.e)
    wq = (w_uq[0] * qk_scale).reshape(Q_LORA, N_HEADS, QK_NOPE + QK_ROPE)
    wq_n = wq[:, :, :QK_NOPE]
    wq_1 = wq[:, :, QK_NOPE:QK_NOPE + ROPE_HALF]
    wq_2 = wq[:, :, QK_NOPE + ROPE_HALF:]
    wq_all = jnp.concatenate([wq_n, wq_1, wq_2, wq_2, wq_1], axis=2).reshape(Q_LORA, Q_COLS)
    w_qkv = jnp.concatenate([wq_all, w_uk[0], w_uv[0]], axis=1).astype(BF16)

    w8 = jnp.broadcast_to(w_dw[0][:, None, :], (CONV_WIDTH, 8, d))
    wco = w_conv_out[0].astype(BF16)
    wao = w_attn_out[0].astype(BF16)
    wout = w_out[0].astype(BF16)
    wfg = w_ffn_gate[0].astype(BF16)
    wfu = w_ffn_up[0].astype(BF16)
    wfd = w_ffn_down[0].astype(BF16)

    cos_all, sin_all = _rope_tables(N_META + seq)
    cos_x, sin_x = cos_all[N_META:], sin_all[N_META:]
    pad_m = ((0, META_ROWS - N_META), (0, 0))
    cos_m, sin_m = jnp.pad(cos_all[:N_META], pad_m), jnp.pad(sin_all[:N_META], pad_m)

    u_m, c_m = _glu_call(meta_pad, g_mix, wa, wb, ba, bb, tm=META_ROWS, tn=512)
    lat_m, kr_m = _latent_call(u_m, w_lat, g_q_lora, g_kv_lora, cos_m, sin_m, tm=META_ROWS)
    qkv_m = _qkv_call(lat_m, w_qkv, cos_m, sin_m, tm=META_ROWS)
    halo0 = jnp.concatenate([jnp.zeros((HALO - N_META, d), F32), c_m[:N_META]], axis=0)

    u, c_pre = _glu_call(xr, g_mix, wa, wb, ba, bb, tm=1024, tn=512)
    gates = _gate_call(u, w_gate, b_gate, tm=1024, tn=1024)
    lat, kr = _latent_call(u, w_lat, g_q_lora, g_kv_lora, cos_x, sin_x, tm=1024)
    qkv = _qkv_call(lat, w_qkv, cos_x, sin_x, tm=1024)
    yc = _conv_call(c_pre, halo0, w8, b_dw, g_conv_ln, b_conv_ln, wco, b_conv_out, gates,
                    batch=batch, seq=seq, tseq=256)
    o = _attn_call(qkv, kr, qkv_m, kr_m, batch=batch, seq=seq, tq=512, tk=512)
    mixed = _mix_call(o, wao, yc, gates, tm=1024, tn=1024)
    h1 = _resid_call(mixed, wout, xr, tm=1024, tn=1024)
    out = _ffn_call(h1, g_ffn, wfg, wfu, wfd, g_final.reshape(1, d), tm=512, tf=512)
    return out.reshape(batch, seq, d)
```

```python
import functools
import math

import jax
import jax.numpy as jnp
from jax import lax
from jax.experimental import pallas as pl
from jax.experimental.pallas import tpu as pltpu

D_MODEL = 2048
N_META = 16
CONV_WIDTH = 31
N_HEADS = 16
Q_LORA = 512
KV_LORA = 512
QK_NOPE = 128
QK_ROPE = 64
V_HEAD = 128
ROPE_THETA = 10000.0
EPS = 1e-6

HEAD_W = 2 * QK_NOPE
ROPE_HALF = QK_ROPE // 2
HALO = 32
META_ROWS = 128
NEG_BIG = -1e30

F32 = jnp.float32
BF16 = jnp.bfloat16

VMEM_LIMIT = 56 * 1024 * 1024


def _cparams(n_axes):
    return pltpu.CompilerParams(dimension_semantics=("arbitrary",) * n_axes,
                                vmem_limit_bytes=VMEM_LIMIT)


def _dot(a, b):
    return jnp.dot(a, b, preferred_element_type=F32)


def _sigmoid(x):
    return 1.0 / (1.0 + jnp.exp(-x))


def _rms_rows(x, g):
    ms = jnp.mean(x * x, axis=-1, keepdims=True)
    return (x * lax.rsqrt(ms + EPS)) * g


def _glu_kernel(x_ref, g_ref, wa_ref, wb_ref, ba_ref, bb_ref, u_ref, c_ref, u_scr, *, tm, rchunk):
    j = pl.program_id(1)

    @pl.when(j == 0)
    def _():
        def body(r, _):
            r0 = pl.multiple_of(r * rchunk, rchunk)
            ub = _rms_rows(x_ref[pl.ds(r0, rchunk), :], g_ref[...]).astype(BF16)
            u_scr[pl.ds(r0, rchunk), :] = ub
            u_ref[pl.ds(r0, rchunk), :] = ub
            return 0
        lax.fori_loop(0, tm // rchunk, body, 0)

    u = u_scr[...]
    za = _dot(u, wa_ref[...]) + ba_ref[...]
    zb = _dot(u, wb_ref[...]) + bb_ref[...]
    c_ref[...] = za * _sigmoid(zb)


def _glu_call(xr, g, wa, wb, ba, bb, *, tm, tn):
    m = xr.shape[0]
    d = D_MODEL
    grid = (m // tm, d // tn)
    return pl.pallas_call(
        functools.partial(_glu_kernel, tm=tm, rchunk=min(tm, 32)),
        grid=grid,
        in_specs=[
            pl.BlockSpec((tm, d), lambda i, j: (i, 0)),
            pl.BlockSpec((1, d), lambda i, j: (0, 0)),
            pl.BlockSpec((d, tn), lambda i, j: (0, j)),
            pl.BlockSpec((d, tn), lambda i, j: (0, j)),
            pl.BlockSpec((1, tn), lambda i, j: (0, j)),
            pl.BlockSpec((1, tn), lambda i, j: (0, j)),
        ],
        out_specs=[
            pl.BlockSpec((tm, d), lambda i, j: (i, 0)),
            pl.BlockSpec((tm, tn), lambda i, j: (i, j)),
        ],
        out_shape=[
            jax.ShapeDtypeStruct((m, d), BF16),
            jax.ShapeDtypeStruct((m, d), F32),
        ],
        scratch_shapes=[pltpu.VMEM((tm, d), BF16)],
        compiler_params=_cparams(2),
        name="glu_in_proj",
    )(xr, g, wa, wb, ba, bb)


def _gate_kernel(u_ref, w_ref, b_ref, o_ref):
    o_ref[...] = _sigmoid(_dot(u_ref[...], w_ref[...]) + b_ref[...]).astype(o_ref.dtype)


def _gate_call(u, w, b, *, tm, tn):
    m, d = u.shape
    n = w.shape[1]
    return pl.pallas_call(
        _gate_kernel,
        grid=(m // tm, n // tn),
        in_specs=[
            pl.BlockSpec((tm, d), lambda i, j: (i, 0)),
            pl.BlockSpec((d, tn), lambda i, j: (0, j)),
            pl.BlockSpec((1, tn), lambda i, j: (0, j)),
        ],
        out_specs=pl.BlockSpec((tm, tn), lambda i, j: (i, j)),
        out_shape=jax.ShapeDtypeStruct((m, n), BF16),
        compiler_params=_cparams(2),
        name="gate_proj",
    )(u, w, b)


def _rope128(v, cos_t, sin_t):
    return v * cos_t + pltpu.roll(v, 2 * ROPE_HALF, 1) * sin_t


def _latent_kernel(u_ref, w_ref, gq_ref, gkv_ref, cos_ref, sin_ref, lat_ref, kr_ref):
    z = _dot(u_ref[...], w_ref[...])
    lat_ref[:, 0:Q_LORA] = _rms_rows(z[:, 0:Q_LORA], gq_ref[...]).astype(BF16)
    lat_ref[:, Q_LORA:Q_LORA + KV_LORA] = _rms_rows(
        z[:, Q_LORA:Q_LORA + KV_LORA], gkv_ref[...]).astype(BF16)
    kr = z[:, Q_LORA + KV_LORA:Q_LORA + KV_LORA + 128]
    kr_ref[...] = _rope128(kr, cos_ref[...], sin_ref[...]).astype(BF16)


def _latent_call(u, w, gq, gkv, cos_t, sin_t, *, tm):
    m, d = u.shape
    n = w.shape[1]
    n_tab = cos_t.shape[0] // tm
    return pl.pallas_call(
        _latent_kernel,
        grid=(m // tm,),
        in_specs=[
            pl.BlockSpec((tm, d), lambda i: (i, 0)),
            pl.BlockSpec((d, n), lambda i: (0, 0)),
            pl.BlockSpec((1, Q_LORA), lambda i: (0, 0)),
            pl.BlockSpec((1, KV_LORA), lambda i: (0, 0)),
            pl.BlockSpec((tm, 128), lambda i: (i % n_tab, 0)),
            pl.BlockSpec((tm, 128), lambda i: (i % n_tab, 0)),
        ],
        out_specs=[
            pl.BlockSpec((tm, Q_LORA + KV_LORA), lambda i: (i, 0)),
            pl.BlockSpec((tm, 128), lambda i: (i, 0)),
        ],
        out_shape=[
            jax.ShapeDtypeStruct((m, Q_LORA + KV_LORA), BF16),
            jax.ShapeDtypeStruct((m, 128), BF16),
        ],
        compiler_params=_cparams(1),
        name="latent_proj",
    )(u, w, gq, gkv, cos_t, sin_t)


Q_COLS = N_HEADS * HEAD_W
KV_COLS = N_HEADS * QK_NOPE
QKV_TN = 1024
Q_STEPS = Q_COLS // QKV_TN
QK_STEPS = (Q_COLS + KV_COLS) // QKV_TN
VT_STEPS = KV_COLS // QKV_TN
NT_DIMS = (((1,), (1,)), ((), ()))


def _qkv_kernel(lat_ref, w_ref, wvt_ref, cos_ref, sin_ref, qk_ref, vt_ref):
    j = pl.program_id(1)

    @pl.when(j < Q_STEPS)
    def _():
        z = _dot(lat_ref[:, 0:Q_LORA], w_ref[...])
        cos_t = cos_ref[...]
        sin_t = sin_ref[...]
        for hh in range(QKV_TN // HEAD_W):
            c0 = hh * HEAD_W
            qk_ref[:, c0:c0 + QK_NOPE] = z[:, c0:c0 + QK_NOPE].astype(BF16)
            qk_ref[:, c0 + QK_NOPE:c0 + HEAD_W] = _rope128(
                z[:, c0 + QK_NOPE:c0 + HEAD_W], cos_t, sin_t).astype(BF16)

    @pl.when(jnp.logical_and(j >= Q_STEPS, j < QK_STEPS))
    def _():
        qk_ref[...] = _dot(lat_ref[:, Q_LORA:Q_LORA + KV_LORA], w_ref[...]).astype(BF16)

    @pl.when(j >= QK_STEPS)
    def _():
        vt_ref[...] = lax.dot_general(wvt_ref[...], lat_ref[:, Q_LORA:Q_LORA + KV_LORA], NT_DIMS,
                                      preferred_element_type=F32).astype(BF16)


def _qkv_call(lat, w, wvt, cos_t, sin_t, *, tm):
    m = lat.shape[0]
    n_tab = cos_t.shape[0] // tm
    return pl.pallas_call(
        _qkv_kernel,
        grid=(m // tm, QK_STEPS + VT_STEPS),
        in_specs=[
            pl.BlockSpec((tm, Q_LORA + KV_LORA), lambda i, j: (i, 0)),
            pl.BlockSpec((Q_LORA, QKV_TN), lambda i, j: (0, jnp.minimum(j, QK_STEPS - 1))),
            pl.BlockSpec((QKV_TN, KV_LORA), lambda i, j: (jnp.maximum(j - QK_STEPS, 0), 0)),
            pl.BlockSpec((tm, 128), lambda i, j: (i % n_tab, 0)),
            pl.BlockSpec((tm, 128), lambda i, j: (i % n_tab, 0)),
        ],
        out_specs=[
            pl.BlockSpec((tm, QKV_TN), lambda i, j: (i, jnp.minimum(j, QK_STEPS - 1))),
            pl.BlockSpec((QKV_TN, tm), lambda i, j: (jnp.maximum(j - QK_STEPS, 0), i)),
        ],
        out_shape=[
            jax.ShapeDtypeStruct((m, Q_COLS + KV_COLS), BF16),
            jax.ShapeDtypeStruct((KV_COLS, m), BF16),
        ],
        compiler_params=_cparams(2),
        name="qkv_up_proj",
    )(lat, w, wvt, cos_t, sin_t)


CONV_R = 64
LN_R = 32
CONV_TN = 512


def _conv_kernel(cur_ref, prev_ref, halo0_ref, w8_ref, bdw_ref, gln_ref, bln_ref, wco_ref, bco_ref,
                 gc_ref, o_ref, xw_scr, conv_scr, c2_scr, *, tseq):
    i = pl.program_id(1)

    @pl.when(i == 0)
    def _():
        xw_scr[0:HALO, :] = halo0_ref[...]

    @pl.when(i > 0)
    def _():
        xw_scr[0:HALO, :] = prev_ref[...]

    xw_scr[HALO:HALO + tseq, :] = cur_ref[...]

    off = HALO - (CONV_WIDTH - 1)

    def col_body(cc, _):
        c0 = pl.multiple_of(cc * 128, 128)

        def row_body(rb, _):
            r0 = pl.multiple_of(rb * CONV_R, CONV_R)
            xv = xw_scr[pl.ds(r0, CONV_R + HALO), pl.ds(c0, 128)]
            acc = jnp.zeros((CONV_R // 8, 8, 128), F32)
            for jj in range(CONV_WIDTH):
                xs = xv[off + jj:off + jj + CONV_R]
                acc = acc + xs.reshape(CONV_R // 8, 8, 128) * w8_ref[jj, :, pl.ds(c0, 128)][None]
            conv_scr[pl.ds(r0, CONV_R), pl.ds(c0, 128)] = acc.reshape(CONV_R, 128)
            return 0

        lax.fori_loop(0, tseq // CONV_R, row_body, 0)
        return 0

    lax.fori_loop(0, D_MODEL // 128, col_body, 0)

    def ln_body(r, _):
        r0 = pl.multiple_of(r * LN_R, LN_R)
        c = conv_scr[pl.ds(r0, LN_R), :] + bdw_ref[...]
        mu = jnp.mean(c, axis=-1, keepdims=True)
        cc = c - mu
        var = jnp.mean(cc * cc, axis=-1, keepdims=True)
        y = cc * lax.rsqrt(var + EPS) * gln_ref[...] + bln_ref[...]
        c2_scr[pl.ds(r0, LN_R), :] = (y * _sigmoid(y)).astype(BF16)
        return 0

    lax.fori_loop(0, tseq // LN_R, ln_body, 0)

    c2 = c2_scr[...]
    for n0 in range(0, D_MODEL, CONV_TN):
        y = _dot(c2, wco_ref[:, n0:n0 + CONV_TN]) + bco_ref[:, n0:n0 + CONV_TN]
        o_ref[:, n0:n0 + CONV_TN] = gc_ref[:, n0:n0 + CONV_TN].astype(F32) * y


def _conv_call(c_pre, halo0, w8, bdw, gln, bln, wco, bco, gates, *, batch, seq, tseq):
    d = D_MODEL
    nt = seq // tseq
    per_halo = tseq // HALO
    return pl.pallas_call(
        functools.partial(_conv_kernel, tseq=tseq),
        grid=(batch, nt),
        in_specs=[
            pl.BlockSpec((tseq, d), lambda b, i: (b * nt + i, 0)),
            pl.BlockSpec((HALO, d), lambda b, i: (jnp.maximum((b * nt + i) * per_halo - 1, 0), 0)),
            pl.BlockSpec((HALO, d), lambda b, i: (0, 0)),
            pl.BlockSpec((CONV_WIDTH, 8, d), lambda b, i: (0, 0, 0)),
            pl.BlockSpec((1, d), lambda b, i: (0, 0)),
            pl.BlockSpec((1, d), lambda b, i: (0, 0)),
            pl.BlockSpec((1, d), lambda b, i: (0, 0)),
            pl.BlockSpec((d, d), lambda b, i: (0, 0)),
            pl.BlockSpec((1, d), lambda b, i: (0, 0)),
            pl.BlockSpec((tseq, d), lambda b, i: (b * nt + i, 0)),
        ],
        out_specs=pl.BlockSpec((tseq, d), lambda b, i: (b * nt + i, 0)),
        out_shape=jax.ShapeDtypeStruct((batch * seq, d), F32),
        scratch_shapes=[
            pltpu.VMEM((tseq + HALO, d), F32),
            pltpu.VMEM((tseq, d), F32),
            pltpu.VMEM((tseq, d), BF16),
        ],
        compiler_params=_cparams(2),
        name="conv_module",
    )(c_pre, c_pre, halo0, w8, bdw, gln, bln, wco, bco, gates)


def _attn_kernel(q_ref, kn_ref, kr_ref, vt_ref, knm_ref, krm_ref, vtm_ref, o_ref, *, seq, tq):
    km = jnp.concatenate([knm_ref[...], krm_ref[...]], axis=1)
    vtm = vtm_ref[...]
    is_meta = lax.broadcasted_iota(jnp.int32, (META_ROWS, tq), 0) < N_META
    causal = (lax.broadcasted_iota(jnp.int32, (tq, tq), 0)
              <= lax.broadcasted_iota(jnp.int32, (tq, tq), 1))
    for i in range(seq // tq):
        lo, hi = i * tq, (i + 1) * tq
        q = q_ref[lo:hi, :]
        k = jnp.concatenate([kn_ref[0:hi, :], kr_ref[0:hi, :]], axis=1)
        s_m = jnp.where(is_meta, lax.dot_general(km, q, NT_DIMS, preferred_element_type=F32), NEG_BIG)
        s_x = lax.dot_general(k, q, NT_DIMS, preferred_element_type=F32)
        parts = [s_m]
        if i:
            parts.append(s_x[0:lo])
        parts.append(jnp.where(causal, s_x[lo:hi], NEG_BIG))
        m = functools.reduce(jnp.maximum, [jnp.max(s, axis=0, keepdims=True) for s in parts])
        probs = [jnp.exp2(s - m) for s in parts]
        l = functools.reduce(jnp.add, [jnp.sum(p, axis=0, keepdims=True) for p in probs])
        p_t = jnp.concatenate([p.astype(BF16) for p in probs], axis=0)
        v_t = jnp.concatenate([vtm, vt_ref[:, 0:hi]], axis=1)
        o_t = _dot(v_t, p_t) * (1.0 / l)
        o_ref[lo:hi, :] = o_t.T.astype(o_ref.dtype)


def _attn_call(qk, kr, vt, qk_m, kr_m, vt_m, *, batch, seq, tq):
    kn_blk0 = Q_COLS // QK_NOPE
    return pl.pallas_call(
        functools.partial(_attn_kernel, seq=seq, tq=tq),
        grid=(batch, N_HEADS),
        in_specs=[
            pl.BlockSpec((seq, HEAD_W), lambda b, h: (b, h)),
            pl.BlockSpec((seq, QK_NOPE), lambda b, h: (b, kn_blk0 + h)),
            pl.BlockSpec((seq, 128), lambda b, h: (b, 0)),
            pl.BlockSpec((V_HEAD, seq), lambda b, h: (h, b)),
            pl.BlockSpec((META_ROWS, QK_NOPE), lambda b, h: (0, kn_blk0 + h)),
            pl.BlockSpec((META_ROWS, 128), lambda b, h: (0, 0)),
            pl.BlockSpec((V_HEAD, META_ROWS), lambda b, h: (h, 0)),
        ],
        out_specs=pl.BlockSpec((seq, V_HEAD), lambda b, h: (b, h)),
        out_shape=jax.ShapeDtypeStruct((batch * seq, N_HEADS * V_HEAD), BF16),
        compiler_params=_cparams(2),
        name="mla_attention",
    )(qk, qk, kr, vt, qk_m, kr_m, vt_m)


def _mix_kernel(o_ref, w_ref, yc_ref, ga_ref, m_ref):
    ya = _dot(o_ref[...], w_ref[...])
    m_ref[...] = (yc_ref[...] + ga_ref[...].astype(F32) * ya).astype(m_ref.dtype)


def _mix_call(o, w, yc, gates, *, tm, tn):
    m, d = o.shape
    n = w.shape[1]
    ga_blk0 = D_MODEL // tn
    return pl.pallas_call(
        _mix_kernel,
        grid=(m // tm, n // tn),
        in_specs=[
            pl.BlockSpec((tm, d), lambda i, j: (i, 0)),
            pl.BlockSpec((d, tn), lambda i, j: (0, j)),
            pl.BlockSpec((tm, tn), lambda i, j: (i, j)),
            pl.BlockSpec((tm, tn), lambda i, j: (i, ga_blk0 + j)),
        ],
        out_specs=pl.BlockSpec((tm, tn), lambda i, j: (i, j)),
        out_shape=jax.ShapeDtypeStruct((m, n), BF16),
        compiler_params=_cparams(2),
        name="attn_out_mix",
    )(o, w, yc, gates)


def _resid_kernel(m_ref, w_ref, x_ref, h_ref):
    h_ref[...] = x_ref[...] + _dot(m_ref[...], w_ref[...])


def _resid_call(mixed, w, xr, *, tm, tn):
    m, d = mixed.shape
    n = w.shape[1]
    return pl.pallas_call(
        _resid_kernel,
        grid=(m // tm, n // tn),
        in_specs=[
            pl.BlockSpec((tm, d), lambda i, j: (i, 0)),
            pl.BlockSpec((d, tn), lambda i, j: (0, j)),
            pl.BlockSpec((tm, tn), lambda i, j: (i, j)),
        ],
        out_specs=pl.BlockSpec((tm, tn), lambda i, j: (i, j)),
        out_shape=jax.ShapeDtypeStruct((m, n), F32),
        compiler_params=_cparams(2),
        name="mixer_out_resid",
    )(mixed, w, xr)


def _ffn_kernel(h_ref, gf_ref, wg_ref, wu_ref, wd_ref, gfin_ref, o_ref, hn_scr, *, tm, rchunk):
    f = pl.program_id(1)
    nf = pl.num_programs(1)

    @pl.when(f == 0)
    def _():
        def body(r, _):
            r0 = pl.multiple_of(r * rchunk, rchunk)
            hn_scr[pl.ds(r0, rchunk), :] = _rms_rows(h_ref[pl.ds(r0, rchunk), :], gf_ref[...]).astype(BF16)
            o_ref[pl.ds(r0, rchunk), :] = jnp.zeros((rchunk, D_MODEL), F32)
            return 0
        lax.fori_loop(0, tm // rchunk, body, 0)

    hn = hn_scr[...]
    g = _dot(hn, wg_ref[...])
    u = _dot(hn, wu_ref[...])
    a = (g * _sigmoid(g) * u).astype(BF16)
    o_ref[...] += _dot(a, wd_ref[...])

    @pl.when(f == nf - 1)
    def _():
        def body(r, _):
            r0 = pl.multiple_of(r * rchunk, rchunk)
            h2 = h_ref[pl.ds(r0, rchunk), :] + o_ref[pl.ds(r0, rchunk), :]
            o_ref[pl.ds(r0, rchunk), :] = _rms_rows(h2, gfin_ref[...])
            return 0
        lax.fori_loop(0, tm // rchunk, body, 0)


def _ffn_call(h1, gf, wg, wu, wd, gfin, *, tm, tf):
    m, d = h1.shape
    dff = wg.shape[1]
    return pl.pallas_call(
        functools.partial(_ffn_kernel, tm=tm, rchunk=32),
        grid=(m // tm, dff // tf),
        in_specs=[
            pl.BlockSpec((tm, d), lambda i, f: (i, 0), pipeline_mode=pl.Buffered(1)),
            pl.BlockSpec((1, d), lambda i, f: (0, 0)),
            pl.BlockSpec((d, tf), lambda i, f: (0, f)),
            pl.BlockSpec((d, tf), lambda i, f: (0, f)),
            pl.BlockSpec((tf, d), lambda i, f: (f, 0)),
            pl.BlockSpec((1, d), lambda i, f: (0, 0)),
        ],
        out_specs=pl.BlockSpec((tm, d), lambda i, f: (i, 0)),
        out_shape=jax.ShapeDtypeStruct((m, d), F32),
        scratch_shapes=[pltpu.VMEM((tm, d), BF16)],
        compiler_params=_cparams(2),
        name="swiglu_ffn",
    )(h1, gf, wg, wu, wd, gfin)


def _rope_tables(n_pos):
    pos = jnp.arange(n_pos, dtype=F32)
    inv_freq = ROPE_THETA ** (-jnp.arange(0, QK_ROPE, 2, dtype=F32) / QK_ROPE)
    ang = pos[:, None] * inv_freq[None, :]
    cos, sin = jnp.cos(ang), jnp.sin(ang)
    zero = jnp.zeros_like(cos)
    cos_t = jnp.concatenate([cos, cos, zero, zero], axis=1)
    sin_t = jnp.concatenate([-sin, sin, zero, zero], axis=1)
    return cos_t, sin_t


def kernel(x, meta_tokens, g_mix, w_in, b_glu, b_gate, w_dw, b_dw, g_conv_ln, b_conv_ln, w_conv_out, b_conv_out, g_q_lora, w_uq, g_kv_lora, w_uk, w_uv, w_attn_out, w_out, g_ffn, w_ffn_gate, w_ffn_up, w_ffn_down, g_final):
    batch, seq, d = x.shape
    assert d == D_MODEL and w_in.shape[0] == 1
    n_rows = batch * seq
    xr = x.reshape(n_rows, d)
    meta_pad = jnp.zeros((META_ROWS, d), F32).at[:N_META].set(meta_tokens.astype(F32))

    c_glu = 2 * d
    w0 = w_in[0]
    wa = w0[:, 0:d].astype(BF16)
    wb = w0[:, d:c_glu].astype(BF16)
    o_cq = c_glu
    o_ckv = o_cq + Q_LORA
    o_kr = o_ckv + KV_LORA
    o_gate = o_kr + QK_ROPE
    wkr1 = w0[:, o_kr:o_kr + ROPE_HALF]
    wkr2 = w0[:, o_kr + ROPE_HALF:o_gate]
    w_lat = jnp.concatenate([w0[:, o_cq:o_kr], wkr1, wkr2, wkr2, wkr1], axis=1).astype(BF16)
    w_gate = w0[:, o_gate:o_gate + 2 * d].astype(BF16)
    ba = b_glu[:, 0:d]
    bb = b_glu[:, d:c_glu]

    qk_scale = (QK_NOPE + QK_ROPE) ** -0.5 * math.log2(math.e)
    wq = (w_uq[0] * qk_scale).reshape(Q_LORA, N_HEADS, QK_NOPE + QK_ROPE)
    wq_n = wq[:, :, :QK_NOPE]
    wq_1 = wq[:, :, QK_NOPE:QK_NOPE + ROPE_HALF]
    wq_2 = wq[:, :, QK_NOPE + ROPE_HALF:]
    wq_all = jnp.concatenate([wq_n, wq_1, wq_2, wq_2, wq_1], axis=2).reshape(Q_LORA, Q_COLS)
    w_qk = jnp.concatenate([wq_all, w_uk[0]], axis=1).astype(BF16)
    w_vt = w_uv[0].T.astype(BF16)

    w8 = jnp.broadcast_to(w_dw[0][:, None, :], (CONV_WIDTH, 8, d))
    wco = w_conv_out[0].astype(BF16)
    wao = w_attn_out[0].astype(BF16)
    wout = w_out[0].astype(BF16)
    wfg = w_ffn_gate[0].astype(BF16)
    wfu = w_ffn_up[0].astype(BF16)
    wfd = w_ffn_down[0].astype(BF16)

    cos_all, sin_all = _rope_tables(N_META + seq)
    cos_x, sin_x = cos_all[N_META:], sin_all[N_META:]
    pad_m = ((0, META_ROWS - N_META), (0, 0))
    cos_m, sin_m = jnp.pad(cos_all[:N_META], pad_m), jnp.pad(sin_all[:N_META], pad_m)

    u_m, c_m = _glu_call(meta_pad, g_mix, wa, wb, ba, bb, tm=META_ROWS, tn=512)
    lat_m, kr_m = _latent_call(u_m, w_lat, g_q_lora, g_kv_lora, cos_m, sin_m, tm=META_ROWS)
    qk_m, vt_m = _qkv_call(lat_m, w_qk, w_vt, cos_m, sin_m, tm=META_ROWS)
    halo0 = jnp.concatenate([jnp.zeros((HALO - N_META, d), F32), c_m[:N_META]], axis=0)

    u, c_pre = _glu_call(xr, g_mix, wa, wb, ba, bb, tm=1024, tn=512)
    gates = _gate_call(u, w_gate, b_gate, tm=1024, tn=1024)
    lat, kr = _latent_call(u, w_lat, g_q_lora, g_kv_lora, cos_x, sin_x, tm=1024)
    qk, vt = _qkv_call(lat, w_qk, w_vt, cos_x, sin_x, tm=1024)
    yc = _conv_call(c_pre, halo0, w8, b_dw, g_conv_ln, b_conv_ln, wco, b_conv_out, gates,
                    batch=batch, seq=seq, tseq=256)
    o = _attn_call(qk, kr, vt, qk_m, kr_m, vt_m, batch=batch, seq=seq, tq=512)
    mixed = _mix_call(o, wao, yc, gates, tm=1024, tn=1024)
    h1 = _resid_call(mixed, wout, xr, tm=1024, tn=1024)
    out = _ffn_call(h1, g_ffn, wfg, wfu, wfd, g_final.reshape(1, d), tm=1024, tf=512)
    return out.reshape(batch, seq, d)
```

```python
import functools
import math

import jax
import jax.numpy as jnp
from jax import lax
from jax.experimental import pallas as pl
from jax.experimental.pallas import tpu as pltpu

D_MODEL = 2048
N_META = 16
CONV_WIDTH = 31
N_HEADS = 16
Q_LORA = 512
KV_LORA = 512
QK_NOPE = 128
QK_ROPE = 64
V_HEAD = 128
ROPE_THETA = 10000.0
EPS = 1e-6

HEAD_W = 2 * QK_NOPE
ROPE_HALF = QK_ROPE // 2
HALO = 32
META_ROWS = 128
NEG_BIG = -1e30

F32 = jnp.float32
BF16 = jnp.bfloat16

VMEM_LIMIT = 56 * 1024 * 1024


def _cparams(n_axes):
    return pltpu.CompilerParams(dimension_semantics=("arbitrary",) * n_axes,
                                vmem_limit_bytes=VMEM_LIMIT)


def _dot(a, b):
    return jnp.dot(a, b, preferred_element_type=F32)


def _sigmoid(x):
    return 0.5 * jnp.tanh(0.5 * x) + 0.5


def _rms_rows(x, g):
    ms = jnp.mean(x * x, axis=-1, keepdims=True)
    return (x * lax.rsqrt(ms + EPS)) * g


def _glu_kernel(x_ref, g_ref, wa_ref, wb_ref, ba_ref, bb_ref, u_ref, c_ref, u_scr, *, tm, rchunk):
    j = pl.program_id(1)

    @pl.when(j == 0)
    def _():
        def body(r, _):
            r0 = pl.multiple_of(r * rchunk, rchunk)
            ub = _rms_rows(x_ref[pl.ds(r0, rchunk), :], g_ref[...]).astype(BF16)
            u_scr[pl.ds(r0, rchunk), :] = ub
            u_ref[pl.ds(r0, rchunk), :] = ub
            return 0
        lax.fori_loop(0, tm // rchunk, body, 0)

    u = u_scr[...]
    za = _dot(u, wa_ref[...]) + ba_ref[...]
    zb = _dot(u, wb_ref[...]) + bb_ref[...]
    c_ref[...] = za * _sigmoid(zb)


def _glu_call(xr, g, wa, wb, ba, bb, *, tm, tn):
    m = xr.shape[0]
    d = D_MODEL
    grid = (m // tm, d // tn)
    return pl.pallas_call(
        functools.partial(_glu_kernel, tm=tm, rchunk=64),
        grid=grid,
        in_specs=[
            pl.BlockSpec((tm, d), lambda i, j: (i, 0)),
            pl.BlockSpec((1, d), lambda i, j: (0, 0)),
            pl.BlockSpec((d, tn), lambda i, j: (0, j)),
            pl.BlockSpec((d, tn), lambda i, j: (0, j)),
            pl.BlockSpec((1, tn), lambda i, j: (0, j)),
            pl.BlockSpec((1, tn), lambda i, j: (0, j)),
        ],
        out_specs=[
            pl.BlockSpec((tm, d), lambda i, j: (i, 0)),
            pl.BlockSpec((tm, tn), lambda i, j: (i, j)),
        ],
        out_shape=[
            jax.ShapeDtypeStruct((m, d), BF16),
            jax.ShapeDtypeStruct((m, d), F32),
        ],
        scratch_shapes=[pltpu.VMEM((tm, d), BF16)],
        compiler_params=_cparams(2),
        name="glu_in_proj",
    )(xr, g, wa, wb, ba, bb)


GATE_TN = 1024
CONV_TN = 512
CONV_R = 64
GC_ROWS = 128


def _conv_rows(stage_ref, w8_ref, c, rr):
    off = HALO - (CONV_WIDTH - 1)
    acc = jnp.zeros((CONV_R // 8, 8, 128), F32)
    for jj in range(CONV_WIDTH):
        r = c * (GC_ROWS + HALO) + rr + off + jj
        xs = stage_ref[r:r + CONV_R, :]
        acc = acc + xs.reshape(CONV_R // 8, 8, 128) * w8_ref[jj, :, c * 128:(c + 1) * 128][None]
    return acc.reshape(CONV_R, 128)


def _gate_conv_kernel(u_ref, wg_ref, bg_ref, cur_ref, prev_ref, halo0_ref, w8_ref, bdw_ref,
                      g_ref, cv_ref, xw_scr, stage_scr, *, tm, tiles_per_seq):
    i = pl.program_id(0)

    @pl.when(i % tiles_per_seq == 0)
    def _():
        xw_scr[0:HALO, :] = halo0_ref[...]

    @pl.when(i % tiles_per_seq != 0)
    def _():
        xw_scr[0:HALO, :] = prev_ref[...]

    xw_scr[HALO:HALO + tm, :] = cur_ref[...]

    def body(rb, _):
        r0 = pl.multiple_of(rb * GC_ROWS, GC_ROWS)
        z = _dot(u_ref[pl.ds(r0, GC_ROWS), :], wg_ref[...]) + bg_ref[...]
        g_ref[pl.ds(r0, GC_ROWS), :] = _sigmoid(z).astype(BF16)
        for c in range(CONV_TN // 128):
            stage_scr[c * (GC_ROWS + HALO):(c + 1) * (GC_ROWS + HALO), :] = (
                xw_scr[pl.ds(r0, GC_ROWS + HALO), c * 128:(c + 1) * 128])
        for c in range(CONV_TN // 128):
            for rr in range(0, GC_ROWS, CONV_R):
                cv_ref[pl.ds(r0 + rr, CONV_R), c * 128:(c + 1) * 128] = (
                    _conv_rows(stage_scr, w8_ref, c, rr) + bdw_ref[:, c * 128:(c + 1) * 128])
        return 0

    lax.fori_loop(0, tm // GC_ROWS, body, 0)


def _gate_conv_call(u, wg, bg, c_pre, halo0, w8, bdw, *, tm, seq):
    m, d = u.shape
    n_gate = wg.shape[1]
    assert n_gate // GATE_TN == d // CONV_TN
    per_halo = tm // HALO
    return pl.pallas_call(
        functools.partial(_gate_conv_kernel, tm=tm, tiles_per_seq=seq // tm),
        grid=(m // tm, n_gate // GATE_TN),
        in_specs=[
            pl.BlockSpec((tm, d), lambda i, j: (i, 0)),
            pl.BlockSpec((d, GATE_TN), lambda i, j: (0, j)),
            pl.BlockSpec((1, GATE_TN), lambda i, j: (0, j)),
            pl.BlockSpec((tm, CONV_TN), lambda i, j: (i, j)),
            pl.BlockSpec((HALO, CONV_TN), lambda i, j: (jnp.maximum(i * per_halo - 1, 0), j)),
            pl.BlockSpec((HALO, CONV_TN), lambda i, j: (0, j)),
            pl.BlockSpec((CONV_WIDTH, 8, CONV_TN), lambda i, j: (0, 0, j)),
            pl.BlockSpec((1, CONV_TN), lambda i, j: (0, j)),
        ],
        out_specs=[
            pl.BlockSpec((tm, GATE_TN), lambda i, j: (i, j)),
            pl.BlockSpec((tm, CONV_TN), lambda i, j: (i, j)),
        ],
        out_shape=[
            jax.ShapeDtypeStruct((m, n_gate), BF16),
            jax.ShapeDtypeStruct((m, d), F32),
        ],
        scratch_shapes=[
            pltpu.VMEM((tm + HALO, CONV_TN), F32),
            pltpu.VMEM((CONV_TN // 128 * (GC_ROWS + HALO), 128), F32),
        ],
        compiler_params=_cparams(2),
        name="gate_proj_dwconv",
    )(u, wg, bg, c_pre, c_pre, halo0, w8, bdw)


def _rope128(v, cos_t, sin_t):
    return v * cos_t + pltpu.roll(v, 2 * ROPE_HALF, 1) * sin_t


def _latent_kernel(u_ref, w_ref, gq_ref, gkv_ref, cos_ref, sin_ref, lat_ref, kr_ref):
    z = _dot(u_ref[...], w_ref[...])
    lat_ref[:, 0:Q_LORA] = _rms_rows(z[:, 0:Q_LORA], gq_ref[...]).astype(BF16)
    lat_ref[:, Q_LORA:Q_LORA + KV_LORA] = _rms_rows(
        z[:, Q_LORA:Q_LORA + KV_LORA], gkv_ref[...]).astype(BF16)
    kr = z[:, Q_LORA + KV_LORA:Q_LORA + KV_LORA + 128]
    kr_ref[...] = _rope128(kr, cos_ref[...], sin_ref[...]).astype(BF16)


def _latent_call(u, w, gq, gkv, cos_t, sin_t, *, tm):
    m, d = u.shape
    n = w.shape[1]
    n_tab = cos_t.shape[0] // tm
    return pl.pallas_call(
        _latent_kernel,
        grid=(m // tm,),
        in_specs=[
            pl.BlockSpec((tm, d), lambda i: (i, 0)),
            pl.BlockSpec((d, n), lambda i: (0, 0)),
            pl.BlockSpec((1, Q_LORA), lambda i: (0, 0)),
            pl.BlockSpec((1, KV_LORA), lambda i: (0, 0)),
            pl.BlockSpec((tm, 128), lambda i: (i % n_tab, 0)),
            pl.BlockSpec((tm, 128), lambda i: (i % n_tab, 0)),
        ],
        out_specs=[
            pl.BlockSpec((tm, Q_LORA + KV_LORA), lambda i: (i, 0)),
            pl.BlockSpec((tm, 128), lambda i: (i, 0)),
        ],
        out_shape=[
            jax.ShapeDtypeStruct((m, Q_LORA + KV_LORA), BF16),
            jax.ShapeDtypeStruct((m, 128), BF16),
        ],
        compiler_params=_cparams(1),
        name="latent_proj",
    )(u, w, gq, gkv, cos_t, sin_t)


Q_COLS = N_HEADS * HEAD_W
KV_COLS = N_HEADS * QK_NOPE
QKV_TN = 1024
Q_STEPS = Q_COLS // QKV_TN
QK_STEPS = (Q_COLS + KV_COLS) // QKV_TN
VT_STEPS = KV_COLS // QKV_TN
NT_DIMS = (((1,), (1,)), ((), ()))


def _qkv_kernel(lat_ref, w_ref, wvt_ref, cos_ref, sin_ref, qk_ref, vt_ref):
    j = pl.program_id(1)

    @pl.when(j < Q_STEPS)
    def _():
        z = _dot(lat_ref[:, 0:Q_LORA], w_ref[...])
        cos_t = cos_ref[...]
        sin_t = sin_ref[...]
        for hh in range(QKV_TN // HEAD_W):
            c0 = hh * HEAD_W
            qk_ref[:, c0:c0 + QK_NOPE] = z[:, c0:c0 + QK_NOPE].astype(BF16)
            qk_ref[:, c0 + QK_NOPE:c0 + HEAD_W] = _rope128(
                z[:, c0 + QK_NOPE:c0 + HEAD_W], cos_t, sin_t).astype(BF16)

    @pl.when(jnp.logical_and(j >= Q_STEPS, j < QK_STEPS))
    def _():
        qk_ref[...] = _dot(lat_ref[:, Q_LORA:Q_LORA + KV_LORA], w_ref[...]).astype(BF16)

    @pl.when(j >= QK_STEPS)
    def _():
        vt_ref[...] = lax.dot_general(wvt_ref[...], lat_ref[:, Q_LORA:Q_LORA + KV_LORA], NT_DIMS,
                                      preferred_element_type=F32).astype(BF16)


def _qkv_call(lat, w, wvt, cos_t, sin_t, *, tm):
    m = lat.shape[0]
    n_tab = cos_t.shape[0] // tm
    return pl.pallas_call(
        _qkv_kernel,
        grid=(m // tm, QK_STEPS + VT_STEPS),
        in_specs=[
            pl.BlockSpec((tm, Q_LORA + KV_LORA), lambda i, j: (i, 0)),
            pl.BlockSpec((Q_LORA, QKV_TN), lambda i, j: (0, jnp.minimum(j, QK_STEPS - 1))),
            pl.BlockSpec((QKV_TN, KV_LORA), lambda i, j: (jnp.maximum(j - QK_STEPS, 0), 0)),
            pl.BlockSpec((tm, 128), lambda i, j: (i % n_tab, 0)),
            pl.BlockSpec((tm, 128), lambda i, j: (i % n_tab, 0)),
        ],
        out_specs=[
            pl.BlockSpec((tm, QKV_TN), lambda i, j: (i, jnp.minimum(j, QK_STEPS - 1))),
            pl.BlockSpec((QKV_TN, tm), lambda i, j: (jnp.maximum(j - QK_STEPS, 0), i)),
        ],
        out_shape=[
            jax.ShapeDtypeStruct((m, Q_COLS + KV_COLS), BF16),
            jax.ShapeDtypeStruct((KV_COLS, m), BF16),
        ],
        compiler_params=_cparams(2),
        name="qkv_up_proj",
    )(lat, w, wvt, cos_t, sin_t)


LN_ROWS = 128


def _conv_out_kernel(cv_ref, gln_ref, bln_ref, wco_ref, bco_ref, gc_ref, o_ref, *, tm):
    for r0 in range(0, tm, LN_ROWS):
        c = cv_ref[r0:r0 + LN_ROWS, :]
        mu = jnp.mean(c, axis=-1, keepdims=True)
        cc = c - mu
        var = jnp.mean(cc * cc, axis=-1, keepdims=True)
        y = cc * lax.rsqrt(var + EPS) * gln_ref[...] + bln_ref[...]
        c2 = (y * _sigmoid(y)).astype(BF16)
        yo = _dot(c2, wco_ref[...]) + bco_ref[...]
        o_ref[r0:r0 + LN_ROWS, :] = gc_ref[r0:r0 + LN_ROWS, :].astype(F32) * yo


def _conv_out_call(cv, gln, bln, wco, bco, gates, *, tm):
    m, d = cv.shape
    return pl.pallas_call(
        functools.partial(_conv_out_kernel, tm=tm),
        grid=(m // tm,),
        in_specs=[
            pl.BlockSpec((tm, d), lambda i: (i, 0)),
            pl.BlockSpec((1, d), lambda i: (0, 0)),
            pl.BlockSpec((1, d), lambda i: (0, 0)),
            pl.BlockSpec((d, d), lambda i: (0, 0)),
            pl.BlockSpec((1, d), lambda i: (0, 0)),
            pl.BlockSpec((tm, d), lambda i: (i, 0)),
        ],
        out_specs=pl.BlockSpec((tm, d), lambda i: (i, 0)),
        out_shape=jax.ShapeDtypeStruct((m, d), F32),
        compiler_params=_cparams(1),
        name="conv_ln_out_proj",
    )(cv, gln, bln, wco, bco, gates)


def _attn_kernel(q_ref, kn_ref, kr_ref, vt_ref, knm_ref, krm_ref, vtm_ref, o_ref, *, seq, tq):
    km = jnp.concatenate([knm_ref[...], krm_ref[...]], axis=1)
    vtm = vtm_ref[...]
    is_meta = lax.broadcasted_iota(jnp.int32, (META_ROWS, tq), 0) < N_META
    causal = (lax.broadcasted_iota(jnp.int32, (tq, tq), 0)
              <= lax.broadcasted_iota(jnp.int32, (tq, tq), 1))
    for i in range(seq // tq):
        lo, hi = i * tq, (i + 1) * tq
        q = q_ref[lo:hi, :]
        k = jnp.concatenate([kn_ref[0:hi, :], kr_ref[0:hi, :]], axis=1)
        s_m = jnp.where(is_meta, lax.dot_general(km, q, NT_DIMS, preferred_element_type=F32), NEG_BIG)
        s_x = lax.dot_general(k, q, NT_DIMS, preferred_element_type=F32)
        parts = [s_m]
        if i:
            parts.append(s_x[0:lo])
        parts.append(jnp.where(causal, s_x[lo:hi], NEG_BIG))
        m = functools.reduce(jnp.maximum, [jnp.max(s, axis=0, keepdims=True) for s in parts])
        probs = [jnp.exp2(s - m) for s in parts]
        l = functools.reduce(jnp.add, [jnp.sum(p, axis=0, keepdims=True) for p in probs])
        p_t = jnp.concatenate([p.astype(BF16) for p in probs], axis=0)
        v_t = jnp.concatenate([vtm, vt_ref[:, 0:hi]], axis=1)
        o_t = _dot(v_t, p_t) * (1.0 / l)
        o_ref[lo:hi, :] = o_t.T.astype(o_ref.dtype)


def _attn_call(qk, kr, vt, qk_m, kr_m, vt_m, *, batch, seq, tq):
    kn_blk0 = Q_COLS // QK_NOPE
    return pl.pallas_call(
        functools.partial(_attn_kernel, seq=seq, tq=tq),
        grid=(batch, N_HEADS),
        in_specs=[
            pl.BlockSpec((seq, HEAD_W), lambda b, h: (b, h)),
            pl.BlockSpec((seq, QK_NOPE), lambda b, h: (b, kn_blk0 + h)),
            pl.BlockSpec((seq, 128), lambda b, h: (b, 0)),
            pl.BlockSpec((V_HEAD, seq), lambda b, h: (h, b)),
            pl.BlockSpec((META_ROWS, QK_NOPE), lambda b, h: (0, kn_blk0 + h)),
            pl.BlockSpec((META_ROWS, 128), lambda b, h: (0, 0)),
            pl.BlockSpec((V_HEAD, META_ROWS), lambda b, h: (h, 0)),
        ],
        out_specs=pl.BlockSpec((seq, V_HEAD), lambda b, h: (b, h)),
        out_shape=jax.ShapeDtypeStruct((batch * seq, N_HEADS * V_HEAD), BF16),
        compiler_params=_cparams(2),
        name="mla_attention",
    )(qk, qk, kr, vt, qk_m, kr_m, vt_m)


def _mix_kernel(o_ref, w_ref, yc_ref, ga_ref, m_ref):
    ya = _dot(o_ref[...], w_ref[...])
    m_ref[...] = (yc_ref[...] + ga_ref[...].astype(F32) * ya).astype(m_ref.dtype)


def _mix_call(o, w, yc, gates, *, tm, tn):
    m, d = o.shape
    n = w.shape[1]
    ga_blk0 = D_MODEL // tn
    return pl.pallas_call(
        _mix_kernel,
        grid=(m // tm, n // tn),
        in_specs=[
            pl.BlockSpec((tm, d), lambda i, j: (i, 0)),
            pl.BlockSpec((d, tn), lambda i, j: (0, j)),
            pl.BlockSpec((tm, tn), lambda i, j: (i, j)),
            pl.BlockSpec((tm, tn), lambda i, j: (i, ga_blk0 + j)),
        ],
        out_specs=pl.BlockSpec((tm, tn), lambda i, j: (i, j)),
        out_shape=jax.ShapeDtypeStruct((m, n), BF16),
        compiler_params=_cparams(2),
        name="attn_out_mix",
    )(o, w, yc, gates)


def _resid_kernel(m_ref, w_ref, x_ref, h_ref):
    h_ref[...] = x_ref[...] + _dot(m_ref[...], w_ref[...])


def _resid_call(mixed, w, xr, *, tm, tn):
    m, d = mixed.shape
    n = w.shape[1]
    return pl.pallas_call(
        _resid_kernel,
        grid=(m // tm, n // tn),
        in_specs=[
            pl.BlockSpec((tm, d), lambda i, j: (i, 0)),
            pl.BlockSpec((d, tn), lambda i, j: (0, j)),
            pl.BlockSpec((tm, tn), lambda i, j: (i, j)),
        ],
        out_specs=pl.BlockSpec((tm, tn), lambda i, j: (i, j)),
        out_shape=jax.ShapeDtypeStruct((m, n), F32),
        compiler_params=_cparams(2),
        name="mixer_out_resid",
    )(mixed, w, xr)


def _ffn_kernel(h_ref, gf_ref, wg_ref, wu_ref, wd_ref, gfin_ref, o_ref, hn_scr, *, tm, rchunk):
    f = pl.program_id(1)
    nf = pl.num_programs(1)

    @pl.when(f == 0)
    def _():
        def body(r, _):
            r0 = pl.multiple_of(r * rchunk, rchunk)
            hn_scr[pl.ds(r0, rchunk), :] = _rms_rows(h_ref[pl.ds(r0, rchunk), :], gf_ref[...]).astype(BF16)
            o_ref[pl.ds(r0, rchunk), :] = jnp.zeros((rchunk, D_MODEL), F32)
            return 0
        lax.fori_loop(0, tm // rchunk, body, 0)

    hn = hn_scr[...]
    g = _dot(hn, wg_ref[...])
    u = _dot(hn, wu_ref[...])
    a = (g * _sigmoid(g) * u).astype(BF16)
    o_ref[...] += _dot(a, wd_ref[...])

    @pl.when(f == nf - 1)
    def _():
        def body(r, _):
            r0 = pl.multiple_of(r * rchunk, rchunk)
            h2 = h_ref[pl.ds(r0, rchunk), :] + o_ref[pl.ds(r0, rchunk), :]
            o_ref[pl.ds(r0, rchunk), :] = _rms_rows(h2, gfin_ref[...])
            return 0
        lax.fori_loop(0, tm // rchunk, body, 0)


def _ffn_call(h1, gf, wg, wu, wd, gfin, *, tm, tf):
    m, d = h1.shape
    dff = wg.shape[1]
    return pl.pallas_call(
        functools.partial(_ffn_kernel, tm=tm, rchunk=64),
        grid=(m // tm, dff // tf),
        in_specs=[
            pl.BlockSpec((tm, d), lambda i, f: (i, 0), pipeline_mode=pl.Buffered(1)),
            pl.BlockSpec((1, d), lambda i, f: (0, 0)),
            pl.BlockSpec((d, tf), lambda i, f: (0, f)),
            pl.BlockSpec((d, tf), lambda i, f: (0, f)),
            pl.BlockSpec((tf, d), lambda i, f: (f, 0)),
            pl.BlockSpec((1, d), lambda i, f: (0, 0)),
        ],
        out_specs=pl.BlockSpec((tm, d), lambda i, f: (i, 0)),
        out_shape=jax.ShapeDtypeStruct((m, d), F32),
        scratch_shapes=[pltpu.VMEM((tm, d), BF16)],
        compiler_params=_cparams(2),
        name="swiglu_ffn",
    )(h1, gf, wg, wu, wd, gfin)


def _rope_tables(n_pos):
    pos = jnp.arange(n_pos, dtype=F32)
    inv_freq = ROPE_THETA ** (-jnp.arange(0, QK_ROPE, 2, dtype=F32) / QK_ROPE)
    ang = pos[:, None] * inv_freq[None, :]
    cos, sin = jnp.cos(ang), jnp.sin(ang)
    zero = jnp.zeros_like(cos)
    cos_t = jnp.concatenate([cos, cos, zero, zero], axis=1)
    sin_t = jnp.concatenate([-sin, sin, zero, zero], axis=1)
    return cos_t, sin_t


def kernel(x, meta_tokens, g_mix, w_in, b_glu, b_gate, w_dw, b_dw, g_conv_ln, b_conv_ln, w_conv_out, b_conv_out, g_q_lora, w_uq, g_kv_lora, w_uk, w_uv, w_attn_out, w_out, g_ffn, w_ffn_gate, w_ffn_up, w_ffn_down, g_final):
    batch, seq, d = x.shape
    assert d == D_MODEL and w_in.shape[0] == 1
    n_rows = batch * seq
    xr = x.reshape(n_rows, d)
    meta_pad = jnp.zeros((META_ROWS, d), F32).at[:N_META].set(meta_tokens.astype(F32))

    c_glu = 2 * d
    w0 = w_in[0]
    wa = w0[:, 0:d].astype(BF16)
    wb = w0[:, d:c_glu].astype(BF16)
    o_cq = c_glu
    o_ckv = o_cq + Q_LORA
    o_kr = o_ckv + KV_LORA
    o_gate = o_kr + QK_ROPE
    wkr1 = w0[:, o_kr:o_kr + ROPE_HALF]
    wkr2 = w0[:, o_kr + ROPE_HALF:o_gate]
    w_lat = jnp.concatenate([w0[:, o_cq:o_kr], wkr1, wkr2, wkr2, wkr1], axis=1).astype(BF16)
    w_gate = w0[:, o_gate:o_gate + 2 * d].astype(BF16)
    ba = b_glu[:, 0:d]
    bb = b_glu[:, d:c_glu]

    qk_scale = (QK_NOPE + QK_ROPE) ** -0.5 * math.log2(math.e)
    wq = (w_uq[0] * qk_scale).reshape(Q_LORA, N_HEADS, QK_NOPE + QK_ROPE)
    wq_n = wq[:, :, :QK_NOPE]
    wq_1 = wq[:, :, QK_NOPE:QK_NOPE + ROPE_HALF]
    wq_2 = wq[:, :, QK_NOPE + ROPE_HALF:]
    wq_all = jnp.concatenate([wq_n, wq_1, wq_2, wq_2, wq_1], axis=2).reshape(Q_LORA, Q_COLS)
    w_qk = jnp.concatenate([wq_all, w_uk[0]], axis=1).astype(BF16)
    w_vt = w_uv[0].T.astype(BF16)

    w8 = jnp.broadcast_to(w_dw[0][:, None, :], (CONV_WIDTH, 8, d))
    wco = w_conv_out[0].astype(BF16)
    wao = w_attn_out[0].astype(BF16)
    wout = w_out[0].astype(BF16)
    wfg = w_ffn_gate[0].astype(BF16)
    wfu = w_ffn_up[0].astype(BF16)
    wfd = w_ffn_down[0].astype(BF16)

    cos_all, sin_all = _rope_tables(N_META + seq)
    cos_x, sin_x = cos_all[N_META:], sin_all[N_META:]
    pad_m = ((0, META_ROWS - N_META), (0, 0))
    cos_m, sin_m = jnp.pad(cos_all[:N_META], pad_m), jnp.pad(sin_all[:N_META], pad_m)

    u_m, c_m = _glu_call(meta_pad, g_mix, wa, wb, ba, bb, tm=META_ROWS, tn=512)
    lat_m, kr_m = _latent_call(u_m, w_lat, g_q_lora, g_kv_lora, cos_m, sin_m, tm=META_ROWS)
    qk_m, vt_m = _qkv_call(lat_m, w_qk, w_vt, cos_m, sin_m, tm=META_ROWS)
    halo0 = jnp.concatenate([jnp.zeros((HALO - N_META, d), F32), c_m[:N_META]], axis=0)

    u, c_pre = _glu_call(xr, g_mix, wa, wb, ba, bb, tm=1024, tn=512)
    gates, cv = _gate_conv_call(u, w_gate, b_gate, c_pre, halo0, w8, b_dw, tm=1024, seq=seq)
    lat, kr = _latent_call(u, w_lat, g_q_lora, g_kv_lora, cos_x, sin_x, tm=1024)
    qk, vt = _qkv_call(lat, w_qk, w_vt, cos_x, sin_x, tm=1024)
    yc = _conv_out_call(cv, g_conv_ln, b_conv_ln, wco, b_conv_out, gates, tm=512)
    o = _attn_call(qk, kr, vt, qk_m, kr_m, vt_m, batch=batch, seq=seq, tq=512)
    mixed = _mix_call(o, wao, yc, gates, tm=1024, tn=1024)
    h1 = _resid_call(mixed, wout, xr, tm=1024, tn=1024)
    out = _ffn_call(h1, g_ffn, wfg, wfu, wfd, g_final.reshape(1, d), tm=1024, tf=512)
    return out.reshape(batch, seq, d)
```
